```python
import jax, jax.numpy as jnp
from jax import lax
import numpy as np

D_MODEL = 2048
BATCH = 16
SEQ = 2048
DEPTH = 1

CHUNK = 64
MIX_WIDTH = D_MODEL
POOL_WIDTH = MIX_WIDTH // 2
POOL_WINDOWS = (2, 4, 8, 16)
POOL_GROUP = POOL_WIDTH // len(POOL_WINDOWS)
HGRN_WIDTH = MIX_WIDTH - POOL_WIDTH
HGRN_HEAD_DIM = 128
HGRN_HEADS = HGRN_WIDTH // HGRN_HEAD_DIM
IN_WIDTH = POOL_WIDTH + 4 * HGRN_WIDTH
D_FF = 4 * D_MODEL
EPS = 1e-6

kernel_name = "hybrid_pool_hgrn2_sandwich_block"


def rmsnorm(x, w):
    xf = x.astype(jnp.float32)
    r = xf * lax.rsqrt(jnp.mean(xf * xf, axis=-1, keepdims=True) + EPS)
    return (r * w.astype(jnp.float32)).astype(x.dtype)


def pool_mixer(u, pool_w, pool_b, pool_scale):
    B, S, _ = u.shape
    uf = u.astype(jnp.float32)
    c = jnp.cumsum(uf, axis=1)
    pos = jnp.arange(S)
    outs = []
    for j, w in enumerate(POOL_WINDOWS):
        sl = slice(j * POOL_GROUP, (j + 1) * POOL_GROUP)
        cj = c[..., sl]
        shifted = jnp.pad(cj, ((0, 0), (w, 0), (0, 0)))[:, :S]
        cnt = jnp.minimum(pos + 1, w).astype(jnp.float32)[None, :, None]
        dj = ((cj - shifted) / cnt - uf[..., sl]).astype(u.dtype)
        outs.append(dj @ pool_w[j] + pool_b[j])
    return jnp.concatenate(outs, axis=-1) * pool_scale


def hgrn2_chunk_scan(q, k, v, logf):
    B, S, H, Dk = q.shape
    Dv = v.shape[-1]
    N = S // CHUNK

    def to_chunks(a):
        return a.reshape(B, N, CHUNK, H, a.shape[-1]).transpose(1, 0, 3, 2, 4)

    causal = jnp.tril(jnp.ones((CHUNK, CHUNK), dtype=bool))[None, None, :, :, None]

    def step(state, inp):
        q_c, k_c, v_c, g_c = inp
        b = jnp.cumsum(g_c, axis=2)
        o_inter = jnp.einsum('bhtd,bhde->bhte', q_c * jnp.exp(b), state)
        diff = b[:, :, :, None, :] - b[:, :, None, :, :]
        decay = jnp.exp(jnp.where(causal, diff, -jnp.inf))
        scores = jnp.einsum('bhtd,bhsd,bhtsd->bhts', q_c, k_c, decay)
        o_intra = jnp.einsum('bhts,bhse->bhte', scores, v_c)
        b_last = b[:, :, -1:, :]
        new_state = jnp.exp(b_last[:, :, 0, :])[..., None] * state + jnp.einsum(
            'bhsd,bhse->bhde', k_c * jnp.exp(b_last - b), v_c)
        return new_state, o_inter + o_intra

    init = jnp.zeros((B, H, Dk, Dv), jnp.float32)
    _, o = lax.scan(step, init, (to_chunks(q), to_chunks(k), to_chunks(v), to_chunks(logf)))
    return o.transpose(1, 0, 3, 2, 4).reshape(B, S, H, Dv)


def hgrn2_mixer(q, f_pre, i, g, lb, norm_w):
    B, S, _ = q.shape
    shp = (B, S, HGRN_HEADS, HGRN_HEAD_DIM)
    qf = q.astype(jnp.float32).reshape(shp)
    vf = i.astype(jnp.float32).reshape(shp)
    sig = jax.nn.sigmoid(f_pre.astype(jnp.float32)).reshape(shp)
    lbh = lb.reshape(HGRN_HEADS, HGRN_HEAD_DIM)
    f = lbh + (1.0 - lbh) * sig
    k = (1.0 - lbh) * (1.0 - sig)
    o = hgrn2_chunk_scan(qf, k, vf, jnp.log(f))
    o = o * jax.nn.sigmoid(g.astype(jnp.float32)).reshape(shp)
    o = o * lax.rsqrt(jnp.mean(o * o, axis=-1, keepdims=True) + EPS)
    o = o * norm_w.astype(jnp.float32).reshape(HGRN_HEADS, HGRN_HEAD_DIM)
    return o.reshape(B, S, HGRN_WIDTH).astype(q.dtype)


def setup_inputs(seed: int = 0) -> dict:
    key = jax.random.key(seed)
    ks = jax.random.split(key, 16)
    f32 = jnp.float32
    nrm = lambda k, s, sc: jax.random.normal(k, s, f32) * sc
    gain = lambda k, s: 1.0 + 0.05 * jax.random.normal(k, s, f32)
    return {
        "x": jax.random.normal(ks[0], (BATCH, SEQ, D_MODEL), f32),
        "w_in": nrm(ks[1], (DEPTH, D_MODEL, IN_WIDTH), D_MODEL ** -0.5),
        "pool_w": nrm(ks[2], (DEPTH, len(POOL_WINDOWS), POOL_GROUP, POOL_GROUP), POOL_GROUP ** -0.5),
        "pool_b": nrm(ks[3], (DEPTH, len(POOL_WINDOWS), POOL_GROUP), 0.02),
        "pool_scale": gain(ks[4], (DEPTH, POOL_WIDTH)),
        "lb_logits": nrm(ks[5], (DEPTH + 1, HGRN_WIDTH), 0.5),
        "hgrn_norm_w": gain(ks[6], (DEPTH, HGRN_WIDTH)),
        "w_out": nrm(ks[7], (DEPTH, MIX_WIDTH, D_MODEL), MIX_WIDTH ** -0.5),
        "norm_mix_pre": gain(ks[8], (DEPTH, D_MODEL)),
        "norm_mix_post": gain(ks[9], (DEPTH, D_MODEL)),
        "norm_mlp_pre": gain(ks[10], (DEPTH, D_MODEL)),
        "norm_mlp_post": gain(ks[11], (DEPTH, D_MODEL)),
        "w_mlp_in": nrm(ks[12], (DEPTH, D_MODEL, D_FF), D_MODEL ** -0.5),
        "w_mlp_out": nrm(ks[13], (DEPTH, D_FF, D_MODEL), D_FF ** -0.5),
    }


def reference(x, w_in, pool_w, pool_b, pool_scale, lb_logits, hgrn_norm_w, w_out,
              norm_mix_pre, norm_mix_post, norm_mlp_pre, norm_mlp_post,
              w_mlp_in, w_mlp_out):
    lb_all = jnp.cumsum(jax.nn.softmax(lb_logits.astype(jnp.float32), axis=0), axis=0)
    P, H4 = POOL_WIDTH, HGRN_WIDTH
    for l in range(DEPTH):
        h = rmsnorm(x, norm_mix_pre[l])
        proj = h @ w_in[l]
        u_pool = proj[..., :P]
        q = proj[..., P:P + H4]
        f_pre = proj[..., P + H4:P + 2 * H4]
        i = proj[..., P + 2 * H4:P + 3 * H4]
        g = proj[..., P + 3 * H4:P + 4 * H4]
        y_pool = pool_mixer(u_pool, pool_w[l], pool_b[l], pool_scale[l])
        y_hgrn = hgrn2_mixer(q, f_pre, i, g, lb_all[l], hgrn_norm_w[l])
        mix = jnp.concatenate([y_pool, y_hgrn], axis=-1) @ w_out[l]
        x = x + rmsnorm(mix, norm_mix_post[l])
        h = rmsnorm(x, norm_mlp_pre[l])
        ff = jnp.square(jax.nn.relu(h @ w_mlp_in[l])) @ w_mlp_out[l]
        x = x + rmsnorm(ff, norm_mlp_post[l])
    return x
```

```python
import functools

import jax
import jax.numpy as jnp
from jax import lax
from jax.experimental import pallas as pl
from jax.experimental.pallas import tpu as pltpu

F32 = jnp.float32
BF16 = jnp.bfloat16

EPS = 1e-6
POOL_WINDOWS = (2, 4, 8, 16)
MAX_WINDOW = max(POOL_WINDOWS)
HEAD_DIM = 128
HGRN_CHUNK = 64
LOG2E = 1.4426950408889634

VMEM_LIMIT_BYTES = 56 * 1024 * 1024


def _rms_scale(v):
    return lax.rsqrt(jnp.mean(v * v, axis=-1, keepdims=True) + EPS)


def _resident(shape):
    return pl.BlockSpec(shape, lambda *_: (0,) * len(shape), pipeline_mode=pl.Buffered(1))


def _inproj_kernel(x_ref, g_ref, w_ref, o_ref):
    x = x_ref[...]
    h = (x * _rms_scale(x) * g_ref[...]).astype(BF16)
    o_ref[...] = jnp.dot(h, w_ref[...], preferred_element_type=F32)


def _in_proj(x2d, gain, w_bf16, tm):
    t, d = x2d.shape
    n = w_bf16.shape[1]
    return pl.pallas_call(
        _inproj_kernel,
        grid=(t // tm,),
        in_specs=[
            pl.BlockSpec((tm, d), lambda i: (i, 0)),
            _resident((1, d)),
            _resident((d, n)),
        ],
        out_specs=pl.BlockSpec((tm, n), lambda i: (i, 0)),
        out_shape=jax.ShapeDtypeStruct((t, n), F32),
        compiler_params=pltpu.CompilerParams(
            dimension_semantics=("arbitrary",), vmem_limit_bytes=VMEM_LIMIT_BYTES),
        name="in_proj",
    )(x2d, gain, w_bf16)


def _shift_rows(v, k):
    return pltpu.roll(v, k % v.shape[0], axis=0)


def _boundary_rows(b, row, m):
    c, w = b.shape
    if m == 1:
        return jnp.where((row & 1) == 1, _shift_rows(b, 1), b)
    if m == 2:
        p = row & 3
        return jnp.where(p == 0, _shift_rows(b, -1),
                         jnp.where(p == 1, b,
                                   jnp.where(p == 2, _shift_rows(b, 1), _shift_rows(b, 2))))
    blocks = b.reshape(c // (2 * m), 2 * m, w)
    edge = blocks[:, m - 1:m, :]
    return jnp.broadcast_to(edge, blocks.shape).reshape(c, w)


def _mixer_kernel(p_ref, lbl_ref, poolw_ref, poolb_ref, pscale_ref, normw_ref, o_ref,
                  ubuf, st_ref, *, ts, pool_width, hgrn_width):
    i = pl.program_id(1)
    pw, hw = pool_width, hgrn_width
    group = pw // len(POOL_WINDOWS)
    heads = hw // HEAD_DIM
    c = HGRN_CHUNK

    @pl.when(i == 0)
    def _():
        ubuf[0:MAX_WINDOW, :] = jnp.zeros((MAX_WINDOW, pw), F32)
        st_ref[...] = jnp.zeros_like(st_ref)

    ubuf[MAX_WINDOW:MAX_WINDOW + ts, :] = p_ref[:, 0:pw]
    pos = i * ts + lax.broadcasted_iota(jnp.int32, (ts, 1), 0)
    for j, w in enumerate(POOL_WINDOWS):
        cs = slice(j * group, (j + 1) * group)
        u = ubuf[MAX_WINDOW:MAX_WINDOW + ts, cs]
        s = u
        for k in range(1, w):
            s = s + ubuf[MAX_WINDOW - k:MAX_WINDOW - k + ts, cs]
        cnt = jnp.minimum(pos + 1, w).astype(F32)
        dj = (s / cnt - u).astype(BF16)
        y = jnp.dot(dj, poolw_ref[j], preferred_element_type=F32) + poolb_ref[j]
        o_ref[:, cs] = (y * pscale_ref[:, cs]).astype(o_ref.dtype)
    ubuf[0:MAX_WINDOW, :] = ubuf[ts:ts + MAX_WINDOW, :]

    lbl = lbl_ref[...]
    lbe = jnp.exp(lbl - jnp.max(lbl, axis=0, keepdims=True))
    lb = lbe[0:1, :] / jnp.sum(lbe, axis=0, keepdims=True)

    row = lax.broadcasted_iota(jnp.int32, (c, 1), 0)
    tt = lax.broadcasted_iota(jnp.int32, (c, c), 0)
    ss = lax.broadcasted_iota(jnp.int32, (c, c), 1)
    xor = tt ^ ss
    levels = []
    m = 1
    while m < c:
        levels.append(m)
        m *= 2
    nt_dims = (((1,), (1,)), ((), ()))
    tn_dims = (((0,), (0,)), ((), ()))

    def chunk_body(ci, carry):
        r0 = pl.multiple_of(ci * c, c)
        rows = pl.ds(r0, c)
        q = p_ref[rows, pw:pw + hw]
        fpre = p_ref[rows, pw + hw:pw + 2 * hw]
        v = p_ref[rows, pw + 2 * hw:pw + 3 * hw]
        gate = p_ref[rows, pw + 3 * hw:pw + 4 * hw]

        sig = 1.0 / (1.0 + jnp.exp(-fpre))
        f = lb + (1.0 - lb) * sig
        kin = (1.0 - lb) * (1.0 - sig)
        g = jnp.log(f)
        b = g
        sh = 1
        while sh < c:
            b = b + jnp.where(row >= sh, _shift_rows(b, sh), 0.0)
            sh *= 2
        b_last = b[c - 1:c, :]
        qd = (q * jnp.exp(b)).astype(BF16)
        kd = (kin * jnp.exp(b_last - b)).astype(BF16)
        vb = v.astype(BF16)
        qb = q.astype(BF16)
        kb = kin.astype(BF16)
        xs = []
        for m in levels:
            wgt = jnp.exp(-jnp.abs(b - _boundary_rows(b, row, m)))
            xs.append((jnp.where((row & m) != 0, q, kin) * wgt).astype(BF16))
        out_gate = 1.0 / (1.0 + jnp.exp(-gate))
        dec_last = jnp.exp(b_last)

        for h in range(heads):
            hs = slice(h * HEAD_DIM, (h + 1) * HEAD_DIM)
            st = st_ref[h]
            o = lax.dot_general(qd[:, hs], st.astype(BF16), nt_dims, preferred_element_type=F32)
            scores = jnp.where(
                xor == 0,
                lax.dot_general(qb[:, hs], kb[:, hs], nt_dims, preferred_element_type=F32), 0.0)
            for lvl, m in enumerate(levels):
                xh = xs[lvl][:, hs]
                pm = lax.dot_general(xh, xh, nt_dims, preferred_element_type=F32)
                scores = jnp.where(((xor >> lvl) == 1) & (tt > ss), pm, scores)
            o = o + jnp.dot(scores.astype(BF16), vb[:, hs], preferred_element_type=F32)
            st_ref[h] = st * dec_last[:, hs] + lax.dot_general(
                vb[:, hs], kd[:, hs], tn_dims, preferred_element_type=F32)
            o = o * out_gate[:, hs]
            o = o * _rms_scale(o) * normw_ref[:, hs]
            o_ref[rows, pw + h * HEAD_DIM:pw + (h + 1) * HEAD_DIM] = o.astype(o_ref.dtype)
        return carry

    lax.fori_loop(0, ts // c, chunk_body, 0)


def _mixers(proj3d, lb_logits, pool_w, pool_b, pool_scale, norm_w, ts):
    bsz, s, n = proj3d.shape
    pw = pool_scale.shape[-1]
    hw = norm_w.shape[-1]
    heads = hw // HEAD_DIM
    kern = functools.partial(_mixer_kernel, ts=ts, pool_width=pw, hgrn_width=hw)
    return pl.pallas_call(
        kern,
        grid=(bsz, s // ts),
        in_specs=[
            pl.BlockSpec((None, ts, n), lambda b, i: (b, i, 0)),
            _resident(lb_logits.shape),
            _resident(pool_w.shape),
            _resident(pool_b.shape),
            _resident(pool_scale.shape),
            _resident(norm_w.shape),
        ],
        out_specs=pl.BlockSpec((None, ts, pw + hw), lambda b, i: (b, i, 0)),
        out_shape=jax.ShapeDtypeStruct((bsz, s, pw + hw), BF16),
        scratch_shapes=[
            pltpu.VMEM((MAX_WINDOW + ts, pw), F32),
            pltpu.VMEM((heads, HEAD_DIM, HEAD_DIM), F32),
        ],
        compiler_params=pltpu.CompilerParams(
            dimension_semantics=("arbitrary", "arbitrary"), vmem_limit_bytes=VMEM_LIMIT_BYTES),
        name="mixers",
    )(proj3d, lb_logits, pool_w, pool_b, pool_scale, norm_w)


def _outmlp_kernel(x_ref, mix_ref, wout_ref, gpost_ref, gpre_ref, gpost2_ref, w1_ref, w2_ref,
                   o_ref, x1_sc, h2_sc, acc_sc):
    j = pl.program_id(1)

    @pl.when(j == 0)
    def _():
        mix = jnp.dot(mix_ref[...], wout_ref[...], preferred_element_type=F32)
        x1 = x_ref[...] + mix * _rms_scale(mix) * gpost_ref[...]
        x1_sc[...] = x1
        h2_sc[...] = (x1 * _rms_scale(x1) * gpre_ref[...]).astype(BF16)
        acc_sc[...] = jnp.zeros_like(acc_sc)

    a = jnp.dot(h2_sc[...], w1_ref[...], preferred_element_type=F32)
    a = jnp.square(jnp.maximum(a, 0.0)).astype(BF16)
    acc_sc[...] += jnp.dot(a, w2_ref[...], preferred_element_type=F32)

    @pl.when(j == pl.num_programs(1) - 1)
    def _():
        ff = acc_sc[...]
        o_ref[...] = x1_sc[...] + ff * _rms_scale(ff) * gpost2_ref[...]


def _out_mlp(x2d, mix2d, w_out, g_post, g_pre, g_post2, w1, w2, tm, tf):
    t, d = x2d.shape
    dff = w1.shape[1]
    return pl.pallas_call(
        _outmlp_kernel,
        grid=(t // tm, dff // tf),
        in_specs=[
            pl.BlockSpec((tm, d), lambda i, j: (i, 0)),
            pl.BlockSpec((tm, d), lambda i, j: (i, 0)),
            _resident(w_out.shape),
            _resident((1, d)),
            _resident((1, d)),
            _resident((1, d)),
            pl.BlockSpec((d, tf), lambda i, j: (0, j)),
            pl.BlockSpec((tf, d), lambda i, j: (j, 0)),
        ],
        out_specs=pl.BlockSpec((tm, d), lambda i, j: (i, 0)),
        out_shape=jax.ShapeDtypeStruct((t, d), F32),
        scratch_shapes=[
            pltpu.VMEM((tm, d), F32),
            pltpu.VMEM((tm, d), BF16),
            pltpu.VMEM((tm, d), F32),
        ],
        compiler_params=pltpu.CompilerParams(
            dimension_semantics=("arbitrary", "arbitrary"), vmem_limit_bytes=VMEM_LIMIT_BYTES),
        name="out_mlp",
    )(x2d, mix2d, w_out, g_post, g_pre, g_post2, w1, w2)


def kernel(x, w_in, pool_w, pool_b, pool_scale, lb_logits, hgrn_norm_w, w_out, norm_mix_pre,
           norm_mix_post, norm_mlp_pre, norm_mlp_post, w_mlp_in, w_mlp_out):
    bsz, s, d = x.shape
    depth = w_in.shape[0]
    assert depth == 1 and lb_logits.shape[0] == depth + 1
    x2d = x.reshape(bsz * s, d)
    proj = _in_proj(x2d, norm_mix_pre, w_in[0].astype(BF16), tm=256)
    mix = _mixers(proj.reshape(bsz, s, -1), lb_logits, pool_w[0].astype(BF16),
                  pool_b[0][:, None, :], pool_scale, hgrn_norm_w, ts=256)
    out = _out_mlp(x2d, mix.reshape(bsz * s, -1), w_out[0].astype(BF16), norm_mix_post,
                   norm_mlp_pre, norm_mlp_post, w_mlp_in[0].astype(BF16),
                   w_mlp_out[0].astype(BF16), tm=512, tf=512)
    return out.reshape(bsz, s, d)
```

```python
import functools

import jax
import jax.numpy as jnp
from jax import lax
from jax.experimental import pallas as pl
from jax.experimental.pallas import tpu as pltpu

F32 = jnp.float32
BF16 = jnp.bfloat16

EPS = 1e-6
POOL_WINDOWS = (2, 4, 8, 16)
MAX_WINDOW = max(POOL_WINDOWS)
HEAD_DIM = 128
HGRN_CHUNK = 64
SUBLANES = 8
PROJ_PIECE = 512

VMEM_LIMIT_BYTES = 60 * 1024 * 1024


def _rms_scale(v):
    return lax.rsqrt(jnp.mean(v * v, axis=-1, keepdims=True) + EPS)


def _resident(shape):
    return pl.BlockSpec(shape, lambda *_: (0,) * len(shape), pipeline_mode=pl.Buffered(1))


def _sigmoid(v):
    return 1.0 / (1.0 + jnp.exp(-v))


def _mix_tile(p_ref, o_ref, tile_in_seq, lbl_ref, poolw_ref, poolb_ref, pscale_ref, normw_ref,
              ubuf, st_ref, xbuf, dbuf, tbuf, *, ts, pool_width, hgrn_width, side_work=()):
    i = tile_in_seq
    pw, hw = pool_width, hgrn_width
    group = pw // len(POOL_WINDOWS)
    heads = hw // HEAD_DIM
    c = HGRN_CHUNK
    tiles = c // SUBLANES

    ubuf[MAX_WINDOW:MAX_WINDOW + ts, :] = p_ref[:, 0:pw]
    acc = ubuf[...]
    pos = i * ts + lax.broadcasted_iota(jnp.int32, (ts, group), 0)
    win = 1
    for j, w in enumerate(POOL_WINDOWS):
        while win < w:
            acc = acc + pltpu.roll(acc, win, axis=0)
            win *= 2
        assert win == w
        cs = slice(j * group, (j + 1) * group)
        cnt = jnp.minimum(pos + 1, w).astype(F32)
        dj = (acc[MAX_WINDOW:, 0:group] / cnt - ubuf[MAX_WINDOW:MAX_WINDOW + ts, cs]).astype(BF16)
        y = jnp.dot(dj, poolw_ref[j], preferred_element_type=F32) + poolb_ref[j]
        o_ref[:, cs] = (y * pscale_ref[:, cs]).astype(o_ref.dtype)
        acc = acc[:, group:]
    ubuf[0:MAX_WINDOW, :] = ubuf[ts:ts + MAX_WINDOW, :]

    n_units = 1 + (ts // c) * (heads // 2)
    pending = list(side_work)
    n_side = len(pending)

    def unit_done(unit):
        while pending and n_side - len(pending) < -(-(unit + 1) * n_side // n_units):
            pending.pop(0)()

    unit_done(0)

    lbl = lbl_ref[...]
    lbe = jnp.exp(lbl - jnp.max(lbl, axis=0, keepdims=True))
    lb = lbe[0:1, :] / jnp.sum(lbe, axis=0, keepdims=True)

    sub = lax.broadcasted_iota(jnp.int32, (tiles, SUBLANES, HEAD_DIM), 1)
    tt = lax.broadcasted_iota(jnp.int32, (2 * c, 2 * c), 0)
    ss = lax.broadcasted_iota(jnp.int32, (2 * c, 2 * c), 1)
    xor = tt ^ ss
    n_levels = c.bit_length() - 1
    level_masks = [((xor >> lvl) == 1) & (tt > ss) for lvl in range(n_levels)]
    nt_dims = (((1,), (1,)), ((), ()))
    tn_dims = (((0,), (0,)), ((), ()))

    def col(part, h):
        lo = pw + part * hw + h * HEAD_DIM
        return slice(lo, lo + HEAD_DIM)

    def scale_phase(ci, h):
        rows = slice(ci * c, (ci + 1) * c)
        hs = slice(h * HEAD_DIM, (h + 1) * HEAD_DIM)
        q = p_ref[rows, col(0, h)]
        sig = _sigmoid(p_ref[rows, col(1, h)])
        lbh = lb[:, hs]
        f = lbh + (1.0 - lbh) * sig
        kin = (1.0 - lbh) * (1.0 - sig)

        shape3 = (tiles, SUBLANES, HEAD_DIM)
        qa = (q * f).reshape(shape3)
        kz = kin.reshape(shape3)
        tot = f.reshape(shape3)
        xs = []
        m = 1
        diag = jnp.sum(q * kin, axis=-1, keepdims=True)
        dbuf[ci, :, hs] = jnp.broadcast_to(diag, (c, HEAD_DIM))
        while m < SUBLANES:
            bit = (sub & m) != 0
            x = jnp.where(bit, qa, kz)
            xs.append(x.reshape(c, HEAD_DIM))
            if 2 * m == SUBLANES:
                sib = pltpu.roll(tot, m, axis=1)
            else:
                sib = jnp.where(bit, pltpu.roll(tot, m, axis=1),
                                pltpu.roll(tot, SUBLANES - m, axis=1))
            u = x * sib
            qa = jnp.where(bit, u, qa)
            kz = jnp.where(bit, kz, u)
            tot = tot * sib
            m *= 2
        qa = [qa[r] for r in range(tiles)]
        kz = [kz[r] for r in range(tiles)]
        tot = [tot[r] for r in range(tiles)]
        k = 1
        while k < tiles:
            x = [qa[r] if r & k else kz[r] for r in range(tiles)]
            xs.append(jnp.concatenate(x, axis=0))
            u = [x[r] * tot[r ^ k] for r in range(tiles)]
            qa = [u[r] if r & k else qa[r] for r in range(tiles)]
            kz = [kz[r] if r & k else u[r] for r in range(tiles)]
            tot = [tot[r] * tot[r ^ k] for r in range(tiles)]
            k *= 2
        xs.append(jnp.concatenate(qa, axis=0))
        xs.append(jnp.concatenate(kz, axis=0))
        for slot, x in enumerate(xs):
            xbuf[ci, slot, :, hs] = x.astype(BF16)
        tbuf[ci, :, hs] = tot[0]

    def matmul_phase(ci, pair):
        rows = slice(ci * c, (ci + 1) * c)
        hh = (2 * pair, 2 * pair + 1)
        lanes = [slice(h * HEAD_DIM, (h + 1) * HEAD_DIM) for h in hh]
        stack = lambda parts: jnp.concatenate(parts, axis=0)
        v = stack([p_ref[rows, col(2, h)] for h in hh])
        vb = v.astype(BF16)
        scores = jnp.zeros((2 * c, 2 * c), F32)
        for lvl in range(n_levels):
            xb = stack([xbuf[ci, lvl, :, ls] for ls in lanes])
            pm = lax.dot_general(xb, xb, nt_dims, preferred_element_type=F32)
            scores = jnp.where(level_masks[lvl], pm, scores)
        o = jnp.dot(scores.astype(BF16), vb, preferred_element_type=F32)
        o = o + stack([dbuf[ci, :, ls] for ls in lanes]) * v
        inter = []
        for k, h in enumerate(hh):
            st = st_ref[h]
            inter.append(lax.dot_general(xbuf[ci, n_levels, :, lanes[k]], st.astype(BF16), nt_dims,
                                         preferred_element_type=F32))
            st_ref[h] = st * tbuf[ci, 0:1, lanes[k]] + lax.dot_general(
                vb[k * c:(k + 1) * c], xbuf[ci, n_levels + 1, :, lanes[k]], tn_dims,
                preferred_element_type=F32)
        o = o + stack(inter)
        o = o * _sigmoid(stack([p_ref[rows, col(3, h)] for h in hh]))
        o = o * _rms_scale(o) * stack([jnp.broadcast_to(normw_ref[:, ls], (c, HEAD_DIM))
                                       for ls in lanes])
        for k, h in enumerate(hh):
            o_ref[rows, pw + h * HEAD_DIM:pw + (h + 1) * HEAD_DIM] = (
                o[k * c:(k + 1) * c].astype(o_ref.dtype))

    n_chunks = ts // c
    for h in range(heads):
        scale_phase(0, h)
    for ci in range(n_chunks):
        for pair in range(heads // 2):
            if ci + 1 < n_chunks:
                scale_phase(ci + 1, 2 * pair)
                scale_phase(ci + 1, 2 * pair + 1)
            matmul_phase(ci, pair)
            unit_done(1 + ci * (heads // 2) + pair)


def _inmix_kernel(x_ref, gpre_ref, win_ref, lbl_ref, poolw_ref, poolb_ref, pscale_ref, normw_ref,
                  o_ref, proj0, proj1, hbuf, ubuf, st_ref, xbuf, dbuf, tbuf, *, tiles_per_seq,
                  **dims):
    g = pl.program_id(0)
    tile_in_seq = lax.rem(g - 1, tiles_per_seq)

    @pl.when(g == 0)
    def _():
        proj1[...] = jnp.zeros_like(proj1)

    @pl.when((tile_in_seq == 0) | (g == 0))
    def _():
        ubuf[0:MAX_WINDOW, :] = jnp.zeros((MAX_WINDOW, ubuf.shape[1]), F32)
        st_ref[...] = jnp.zeros_like(st_ref)

    def step(proj_write, proj_read):
        x = x_ref[...]
        hbuf[...] = (x * _rms_scale(x) * gpre_ref[...]).astype(BF16)

        def project(lo):
            def piece():
                cols = slice(lo, lo + PROJ_PIECE)
                proj_write[:, cols] = jnp.dot(hbuf[...], win_ref[:, cols],
                                              preferred_element_type=F32)
            return piece

        pieces = [project(lo) for lo in range(0, win_ref.shape[1], PROJ_PIECE)]
        _mix_tile(proj_read, o_ref, tile_in_seq, lbl_ref, poolw_ref, poolb_ref, pscale_ref,
                  normw_ref, ubuf, st_ref, xbuf, dbuf, tbuf, side_work=pieces, **dims)

    @pl.when(g % 2 == 0)
    def _():
        step(proj0, proj1)

    @pl.when(g % 2 == 1)
    def _():
        step(proj1, proj0)


def _in_mix(x2d, gain, w_in, lb_logits, pool_w, pool_b, pool_scale, norm_w, ts, seq):
    t, d = x2d.shape
    n = w_in.shape[1]
    pw = pool_scale.shape[-1]
    hw = norm_w.shape[-1]
    heads = hw // HEAD_DIM
    n_tiles = t // ts
    n_chunks = ts // HGRN_CHUNK
    n_levels = HGRN_CHUNK.bit_length() - 1
    kern = functools.partial(_inmix_kernel, tiles_per_seq=seq // ts, ts=ts, pool_width=pw,
                             hgrn_width=hw)
    return pl.pallas_call(
        kern,
        grid=(n_tiles + 1,),
        in_specs=[
            pl.BlockSpec((ts, d), lambda g: (jnp.minimum(g, n_tiles - 1), 0)),
            _resident((1, d)),
            _resident((d, n)),
            _resident(lb_logits.shape),
            _resident(pool_w.shape),
            _resident(pool_b.shape),
            _resident(pool_scale.shape),
            _resident(norm_w.shape),
        ],
        out_specs=pl.BlockSpec((ts, pw + hw), lambda g: (jnp.maximum(g - 1, 0), 0)),
        out_shape=jax.ShapeDtypeStruct((t, pw + hw), BF16),
        scratch_shapes=[
            pltpu.VMEM((ts, n), F32),
            pltpu.VMEM((ts, n), F32),
            pltpu.VMEM((ts, d), BF16),
            pltpu.VMEM((MAX_WINDOW + ts, pw), F32),
            pltpu.VMEM((heads, HEAD_DIM, HEAD_DIM), F32),
            pltpu.VMEM((n_chunks, n_levels + 2, HGRN_CHUNK, hw), BF16),
            pltpu.VMEM((n_chunks, HGRN_CHUNK, hw), F32),
            pltpu.VMEM((n_chunks, SUBLANES, hw), F32),
        ],
        compiler_params=pltpu.CompilerParams(
            dimension_semantics=("arbitrary",), vmem_limit_bytes=VMEM_LIMIT_BYTES),
        name="in_mix",
    )(x2d, gain, w_in, lb_logits, pool_w, pool_b, pool_scale, norm_w)


def _outmlp_kernel(x_ref, mix_ref, wout_ref, gpost_ref, gpre_ref, gpost2_ref, w1_ref, w2_ref,
                   o_ref, x1_sc, h2_sc):
    j = pl.program_id(1)

    @pl.when(j == 0)
    def _():
        mix = jnp.dot(mix_ref[...], wout_ref[...], preferred_element_type=F32)
        x1 = x_ref[...] + mix * _rms_scale(mix) * gpost_ref[...]
        x1_sc[...] = x1
        h2_sc[...] = (x1 * _rms_scale(x1) * gpre_ref[...]).astype(BF16)
        o_ref[...] = jnp.zeros_like(o_ref)

    a = jnp.dot(h2_sc[...], w1_ref[...], preferred_element_type=F32)
    a = jnp.square(jnp.maximum(a, 0.0)).astype(BF16)
    o_ref[...] += jnp.dot(a, w2_ref[...], preferred_element_type=F32)

    @pl.when(j == pl.num_programs(1) - 1)
    def _():
        ff = o_ref[...]
        o_ref[...] = x1_sc[...] + ff * _rms_scale(ff) * gpost2_ref[...]


def _out_mlp(x2d, mix2d, w_out, g_post, g_pre, g_post2, w1, w2, tm, tf):
    t, d = x2d.shape
    dff = w1.shape[1]
    return pl.pallas_call(
        _outmlp_kernel,
        grid=(t // tm, dff // tf),
        in_specs=[
            pl.BlockSpec((tm, d), lambda i, j: (i, 0)),
            pl.BlockSpec((tm, d), lambda i, j: (i, 0)),
            _resident(w_out.shape),
            _resident((1, d)),
            _resident((1, d)),
            _resident((1, d)),
            pl.BlockSpec((d, tf), lambda i, j: (0, j)),
            pl.BlockSpec((tf, d), lambda i, j: (j, 0)),
        ],
        out_specs=pl.BlockSpec((tm, d), lambda i, j: (i, 0)),
        out_shape=jax.ShapeDtypeStruct((t, d), F32),
        scratch_shapes=[
            pltpu.VMEM((tm, d), F32),
            pltpu.VMEM((tm, d), BF16),
        ],
        compiler_params=pltpu.CompilerParams(
            dimension_semantics=("arbitrary", "arbitrary"), vmem_limit_bytes=VMEM_LIMIT_BYTES),
        name="out_mlp",
    )(x2d, mix2d, w_out, g_post, g_pre, g_post2, w1, w2)


def kernel(x, w_in, pool_w, pool_b, pool_scale, lb_logits, hgrn_norm_w, w_out, norm_mix_pre,
           norm_mix_post, norm_mlp_pre, norm_mlp_post, w_mlp_in, w_mlp_out):
    bsz, s, d = x.shape
    depth = w_in.shape[0]
    assert depth == 1 and lb_logits.shape[0] == depth + 1
    x2d = x.reshape(bsz * s, d)
    mix = _in_mix(x2d, norm_mix_pre, w_in[0].astype(BF16), lb_logits, pool_w[0].astype(BF16),
                  pool_b[0][:, None, :], pool_scale, hgrn_norm_w, ts=256, seq=s)
    out = _out_mlp(x2d, mix, w_out[0].astype(BF16), norm_mix_post,
                   norm_mlp_pre, norm_mlp_post, w_mlp_in[0].astype(BF16),
                   w_mlp_out[0].astype(BF16), tm=512, tf=1024)
    return out.reshape(bsz, s, d)
```
